```python
import jax, jax.numpy as jnp
from jax import lax
import numpy as np

D_MODEL = 2048
BATCH = 4
SEQ = 4096
DEPTH = 1

CONV_WIDTH = D_MODEL // 2
CONV_KERNEL = 31
HGRN_WIDTH = D_MODEL - CONV_WIDTH
HGRN_HEADS = 8
HGRN_KEY_DIM = 128
HGRN_VAL_DIM = HGRN_WIDTH // HGRN_HEADS
HGRN_FORGET = HGRN_HEADS * HGRN_KEY_DIM
CHUNK = 64
D_FF = 4 * D_MODEL
LN_EPS = 1e-5
ALPHA = (2.0 * DEPTH) ** 0.25
BETA = (8.0 * DEPTH) ** -0.25
IN_COLS = 2 * CONV_WIDTH + 3 * HGRN_FORGET + 2 * HGRN_WIDTH
SPLITS = (CONV_WIDTH,
          2 * CONV_WIDTH,
          2 * CONV_WIDTH + HGRN_FORGET,
          2 * CONV_WIDTH + HGRN_FORGET + HGRN_WIDTH,
          2 * CONV_WIDTH + 2 * HGRN_FORGET + HGRN_WIDTH,
          2 * CONV_WIDTH + 3 * HGRN_FORGET + HGRN_WIDTH)

kernel_name = "bidir_conformer_hgrn2_deepnorm_block"


def layer_norm(x, g, b):
    xf = x.astype(jnp.float32)
    mu = jnp.mean(xf, axis=-1, keepdims=True)
    var = jnp.mean(jnp.square(xf - mu), axis=-1, keepdims=True)
    y = (xf - mu) * lax.rsqrt(var + LN_EPS) * g.astype(jnp.float32) + b.astype(jnp.float32)
    return y.astype(x.dtype)


def conformer_conv(val, gate, w_dw, b_dw, ln_g, ln_b):
    u = val * jax.nn.sigmoid(gate)
    pad = CONV_KERNEL // 2
    y = lax.conv_general_dilated(
        u, w_dw[:, None, :].astype(u.dtype), window_strides=(1,),
        padding=[(pad, pad)], dimension_numbers=('NWC', 'WIO', 'NWC'),
        feature_group_count=CONV_WIDTH)
    y = layer_norm(y + b_dw.astype(y.dtype), ln_g, ln_b)
    return jax.nn.silu(y)


def hgrn2_scan(q, k, v, log_f):
    bsz, nh, length, kd = q.shape
    vd = v.shape[-1]
    n_chunks = length // CHUNK

    def to_chunks(t):
        return t.reshape(bsz, nh, n_chunks, CHUNK, t.shape[-1]).transpose(2, 0, 1, 3, 4)

    lower = jnp.tril(jnp.ones((CHUNK, CHUNK), dtype=bool))[:, :, None]

    def step(state, inp):
        qi, ki, vi, gi = inp
        cum = jnp.cumsum(gi, axis=-2)
        diff = cum[..., :, None, :] - cum[..., None, :, :]
        decay = jnp.exp(jnp.where(lower, diff, -jnp.inf))
        scores = jnp.einsum('bhtsk,bhsk->bhts', qi[..., :, None, :] * decay, ki)
        out = (jnp.einsum('bhts,bhsv->bhtv', scores, vi)
               + jnp.einsum('bhtk,bhkv->bhtv', qi * jnp.exp(cum), state))
        last = cum[..., -1:, :]
        state = (jnp.exp(last[..., 0, :])[..., None] * state
                 + jnp.einsum('bhsk,bhsv->bhkv', ki * jnp.exp(last - cum), vi))
        return state, out

    s0 = jnp.zeros((bsz, nh, kd, vd), jnp.float32)
    _, out = lax.scan(step, s0, (to_chunks(q), to_chunks(k), to_chunks(v), to_chunks(log_f)))
    return out.transpose(1, 2, 0, 3, 4).reshape(bsz, nh, length, vd)


def hgrn2_bidir(q_raw, i_raw, ff_raw, fb_raw, og_raw, lb_fwd, lb_bwd, norm_w):
    bsz, length, _ = q_raw.shape

    def heads(t, d):
        return t.reshape(bsz, length, HGRN_HEADS, d).transpose(0, 2, 1, 3)

    q = heads(jax.nn.silu(q_raw.astype(jnp.float32)), HGRN_KEY_DIM)
    v = heads(i_raw.astype(jnp.float32), HGRN_VAL_DIM)

    def forget(f_raw, lb):
        fr = f_raw.astype(jnp.float32)
        f = lb + (1.0 - lb) * jax.nn.sigmoid(fr)
        k = (1.0 - lb) * jax.nn.sigmoid(-fr)
        return heads(k, HGRN_KEY_DIM), heads(jnp.log(f), HGRN_KEY_DIM)

    k_f, lf_f = forget(ff_raw, lb_fwd.astype(jnp.float32))
    k_b, lf_b = forget(fb_raw, lb_bwd.astype(jnp.float32))
    flip = lambda t: jnp.flip(t, axis=2)
    o = hgrn2_scan(q, k_f, v, lf_f) + flip(hgrn2_scan(flip(q), flip(k_b), flip(v), flip(lf_b)))
    o = o * lax.rsqrt(jnp.mean(jnp.square(o), axis=-1, keepdims=True) + LN_EPS) \
        * norm_w.astype(jnp.float32)
    o = o.transpose(0, 2, 1, 3).reshape(bsz, length, HGRN_WIDTH)
    return (o * jax.nn.silu(og_raw.astype(jnp.float32))).astype(q_raw.dtype)


def setup_inputs(seed: int = 0) -> dict:
    key = jax.random.key(seed)
    ks = jax.random.split(key, 20)
    f32 = jnp.float32
    nrm = lambda k, shape, s: jax.random.normal(k, shape, f32) * s
    return {
        "x": jax.random.normal(ks[0], (BATCH, SEQ, D_MODEL), f32),
        "emb_ln_g": 1.0 + nrm(ks[1], (D_MODEL,), 0.02),
        "emb_ln_b": nrm(ks[2], (D_MODEL,), 0.02),
        "lb_logits": nrm(ks[3], (2, DEPTH + 1, HGRN_FORGET), 0.5),
        "w_in": nrm(ks[4], (DEPTH, D_MODEL, IN_COLS), D_MODEL ** -0.5),
        "conv_w": nrm(ks[5], (DEPTH, CONV_KERNEL, CONV_WIDTH), CONV_KERNEL ** -0.5),
        "conv_b": nrm(ks[6], (DEPTH, CONV_WIDTH), 0.01),
        "conv_ln_g": 1.0 + nrm(ks[7], (DEPTH, CONV_WIDTH), 0.02),
        "conv_ln_b": nrm(ks[8], (DEPTH, CONV_WIDTH), 0.02),
        "hgrn_norm_w": 1.0 + nrm(ks[9], (DEPTH, HGRN_VAL_DIM), 0.02),
        "w_out": nrm(ks[10], (DEPTH, D_MODEL, D_MODEL), BETA * D_MODEL ** -0.5),
        "ln1_g": 1.0 + nrm(ks[11], (DEPTH, D_MODEL), 0.02),
        "ln1_b": nrm(ks[12], (DEPTH, D_MODEL), 0.02),
        "w_up": nrm(ks[13], (DEPTH, D_MODEL, D_FF), D_MODEL ** -0.5),
        "w_down": nrm(ks[14], (DEPTH, D_FF, D_MODEL), BETA * D_FF ** -0.5),
        "ln2_g": 1.0 + nrm(ks[15], (DEPTH, D_MODEL), 0.02),
        "ln2_b": nrm(ks[16], (DEPTH, D_MODEL), 0.02),
    }


def reference(x, emb_ln_g, emb_ln_b, lb_logits, w_in, conv_w, conv_b, conv_ln_g, conv_ln_b,
              hgrn_norm_w, w_out, ln1_g, ln1_b, w_up, w_down, ln2_g, ln2_b):
    lower_bounds = jnp.cumsum(jax.nn.softmax(lb_logits.astype(jnp.float32), axis=1), axis=1)
    h = layer_norm(x, emb_ln_g, emb_ln_b)
    for l in range(DEPTH):
        proj = h @ w_in[l]
        c_val, c_gate, q_raw, i_raw, ff_raw, fb_raw, og_raw = jnp.split(proj, SPLITS, axis=-1)
        conv_out = conformer_conv(c_val, c_gate, conv_w[l], conv_b[l], conv_ln_g[l], conv_ln_b[l])
        rec_out = hgrn2_bidir(q_raw, i_raw, ff_raw, fb_raw, og_raw,
                              lower_bounds[0, l], lower_bounds[1, l], hgrn_norm_w[l])
        mix = jnp.concatenate([conv_out, rec_out], axis=-1) @ w_out[l]
        h = layer_norm(ALPHA * h + mix, ln1_g[l], ln1_b[l])
        ffn = jnp.square(jax.nn.relu(h @ w_up[l])) @ w_down[l]
        h = layer_norm(ALPHA * h + ffn, ln2_g[l], ln2_b[l])
    return h
```

```python
import functools

import jax
import jax.numpy as jnp
from jax import lax
from jax.experimental import pallas as pl
from jax.experimental.pallas import tpu as pltpu

LN_EPS = 1e-5
CONV_KERNEL = 31
CONV_HALO = 16
HEAD_DIM = 128
LANES = 128
SUBLANES = 8
VMEM_LIMIT = 56 * 1024 * 1024

F32 = jnp.float32
BF16 = jnp.bfloat16


def _sigmoid(x):
    return 1.0 / (1.0 + jnp.exp(-x))


def _layer_norm(x, g, b):
    mu = jnp.mean(x, axis=-1, keepdims=True)
    xc = x - mu
    var = jnp.mean(xc * xc, axis=-1, keepdims=True)
    return xc * lax.rsqrt(var + LN_EPS) * g + b


def _dot(a, b):
    return jnp.dot(a, b, preferred_element_type=F32)


def _dot_nt(a, b):
    return lax.dot_general(a, b, (((1,), (1,)), ((), ())), preferred_element_type=F32)


def _dot_tn(a, b):
    return lax.dot_general(a, b, (((0,), (0,)), ((), ())), preferred_element_type=F32)


def _ln_proj_kernel(x_ref, g_ref, b_ref, wa_ref, wb_ref, u_ref, p_ref, h_ref, *, n_conv):
    j = pl.program_id(1)

    @pl.when(j == 0)
    def _():
        h_ref[...] = _layer_norm(x_ref[...], g_ref[...], b_ref[...]).astype(BF16)

    @pl.when(j < n_conv)
    def _():
        h = h_ref[...]
        val = _dot(h, wa_ref[...])
        gate = _dot(h, wb_ref[...])
        u_ref[...] = (val * _sigmoid(gate)).astype(BF16)

    @pl.when(j >= n_conv)
    def _():
        p = _dot(h_ref[...], wa_ref[...])
        for c in range(p_ref.shape[1]):
            p_ref[0, c] = p[:, c * LANES:(c + 1) * LANES].astype(BF16)


def _ln_proj(x2d, g, b, w_in, *, batch, seq, conv_w):
    m, d = x2d.shape
    in_cols = w_in.shape[1]
    tm = min(1024, seq)
    tn = min(512, conv_w)
    n_conv = conv_w // tn
    n_rec = (in_cols - 2 * conv_w) // tn
    rec_blocks = (in_cols - 2 * conv_w) // LANES
    tiles_per_seq = seq // tm

    def wa_map(i, j):
        return (0, jnp.where(j < n_conv, j, j + n_conv))

    def wb_map(i, j):
        return (0, jnp.where(j < n_conv, j + n_conv, 2 * n_conv - 1))

    def u_map(i, j):
        return (i, jnp.minimum(j, n_conv - 1))

    def p_map(i, j):
        return (i // tiles_per_seq, jnp.maximum(j - n_conv, 0), i % tiles_per_seq, 0)

    return pl.pallas_call(
        functools.partial(_ln_proj_kernel, n_conv=n_conv),
        grid=(m // tm, n_conv + n_rec),
        in_specs=[
            pl.BlockSpec((tm, d), lambda i, j: (i, 0)),
            pl.BlockSpec((1, d), lambda i, j: (0, 0)),
            pl.BlockSpec((1, d), lambda i, j: (0, 0)),
            pl.BlockSpec((d, tn), wa_map),
            pl.BlockSpec((d, tn), wb_map),
        ],
        out_specs=[
            pl.BlockSpec((tm, tn), u_map),
            pl.BlockSpec((1, tn // LANES, tm, LANES), p_map),
        ],
        out_shape=[
            jax.ShapeDtypeStruct((m, conv_w), BF16),
            jax.ShapeDtypeStruct((batch, rec_blocks, seq, LANES), BF16),
        ],
        scratch_shapes=[pltpu.VMEM((tm, d), BF16)],
        compiler_params=pltpu.CompilerParams(
            dimension_semantics=("arbitrary", "arbitrary"), vmem_limit_bytes=VMEM_LIMIT),
        name="ln_proj",
    )(x2d, g, b, w_in, w_in)


def _conv_kernel(prev_ref, main_ref, next_ref, w_ref, cb_ref, g_ref, b_ref, o_ref, pad_ref,
                 *, lt, rows):
    i = pl.program_id(1)
    last = pl.num_programs(1) - 1
    n_pad = lt + 2 * CONV_HALO
    pad_ref[0, 0:CONV_HALO, :] = jnp.where(i > 0, prev_ref[0].astype(F32), 0.0)
    pad_ref[0, CONV_HALO:CONV_HALO + lt, :] = main_ref[0].astype(F32)
    pad_ref[0, CONV_HALO + lt:n_pad, :] = jnp.where(i < last, next_ref[0].astype(F32), 0.0)
    for s in range(1, SUBLANES):
        pad_ref[s, 0:n_pad - SUBLANES, :] = pad_ref[0, s:s + n_pad - SUBLANES, :]
    first_tap = CONV_HALO - CONV_KERNEL // 2

    def body(r, carry):
        r0 = pl.multiple_of(r * rows, rows)
        acc = jnp.zeros((rows, pad_ref.shape[2]), F32)
        for k in range(CONV_KERNEL):
            off = first_tap + k
            start = pl.multiple_of(r0 + (off // SUBLANES) * SUBLANES, SUBLANES)
            acc = acc + w_ref[k:k + 1, :] * pad_ref[off % SUBLANES, pl.ds(start, rows), :]
        y = _layer_norm(acc + cb_ref[...], g_ref[...], b_ref[...])
        o_ref[0, pl.ds(r0, rows), :] = (y * _sigmoid(y)).astype(BF16)
        return carry

    lax.fori_loop(0, lt // rows, body, 0)


def _conv(u3d, w, cb, g, b):
    batch, seq, c = u3d.shape
    lt = min(512, seq)
    rows = 32
    halo_blocks = lt // CONV_HALO
    n_halo = seq // CONV_HALO

    return pl.pallas_call(
        functools.partial(_conv_kernel, lt=lt, rows=rows),
        grid=(batch, seq // lt),
        in_specs=[
            pl.BlockSpec((1, CONV_HALO, c), lambda bi, i: (bi, jnp.maximum(i * halo_blocks - 1, 0), 0)),
            pl.BlockSpec((1, lt, c), lambda bi, i: (bi, i, 0)),
            pl.BlockSpec((1, CONV_HALO, c),
                         lambda bi, i: (bi, jnp.minimum((i + 1) * halo_blocks, n_halo - 1), 0)),
            pl.BlockSpec((CONV_KERNEL, c), lambda bi, i: (0, 0)),
            pl.BlockSpec((1, c), lambda bi, i: (0, 0)),
            pl.BlockSpec((1, c), lambda bi, i: (0, 0)),
            pl.BlockSpec((1, c), lambda bi, i: (0, 0)),
        ],
        out_specs=pl.BlockSpec((1, lt, c), lambda bi, i: (bi, i, 0)),
        out_shape=jax.ShapeDtypeStruct((batch, seq, c), BF16),
        scratch_shapes=[pltpu.VMEM((SUBLANES, lt + 2 * CONV_HALO, c), F32)],
        compiler_params=pltpu.CompilerParams(
            dimension_semantics=("arbitrary", "arbitrary"), vmem_limit_bytes=VMEM_LIMIT),
        name="conv",
    )(u3d, u3d, u3d, w, cb, g, b)


def _split(x, hs):
    t, n = x.shape
    x4 = x.reshape(t // (2 * hs), 2, hs, n)
    return x4[:, 0], x4[:, 1]


def _merge(lo, up):
    nb, hs, n = lo.shape
    return jnp.stack([lo, up], axis=1).reshape(nb * 2 * hs, n)


def _hgrn_chunk(q, kf, ff, kb, fb):
    t = q.shape[0]
    row = lax.broadcasted_iota(jnp.int32, q.shape, 0)
    pp, qq, fp = ff, jnp.ones_like(ff), ff
    sb, xb, fbp = fb, jnp.ones_like(fb), fb
    zero = jnp.zeros_like(q)
    levels = []
    hs = 1
    while hs < SUBLANES:
        upper = (row & hs) != 0
        lhs = jnp.concatenate([jnp.where(upper, q * pp, zero), jnp.where(upper, zero, q * sb)], axis=1)
        rhs = jnp.concatenate([jnp.where(upper, zero, kf * qq), jnp.where(upper, kb * xb, zero)], axis=1)
        levels.append((lhs.astype(BF16), rhs.astype(BF16)))
        sib_f = jnp.where(upper, pltpu.roll(fp, hs, 0), pltpu.roll(fp, t - hs, 0))
        sib_b = jnp.where(upper, pltpu.roll(fbp, hs, 0), pltpu.roll(fbp, t - hs, 0))
        pp = jnp.where(upper, pp * sib_f, pp)
        qq = jnp.where(upper, qq, qq * sib_f)
        sb = jnp.where(upper, sb, sb * sib_b)
        xb = jnp.where(upper, xb * sib_b, xb)
        fp = fp * sib_f
        fbp = fbp * sib_b
        hs *= 2
    while hs < t:
        q_lo, q_up = _split(q, hs)
        kf_lo, _ = _split(kf, hs)
        _, kb_up = _split(kb, hs)
        pp_lo, pp_up = _split(pp, hs)
        qq_lo, qq_up = _split(qq, hs)
        sb_lo, sb_up = _split(sb, hs)
        xb_lo, xb_up = _split(xb, hs)
        z = jnp.zeros_like(q_lo)
        lhs = _merge(jnp.concatenate([z, q_lo * sb_lo], axis=-1),
                     jnp.concatenate([q_up * pp_up, z], axis=-1))
        rhs = _merge(jnp.concatenate([kf_lo * qq_lo, z], axis=-1),
                     jnp.concatenate([z, kb_up * xb_up], axis=-1))
        levels.append((lhs.astype(BF16), rhs.astype(BF16)))
        f_lo = pp_lo[:, hs - 1:hs, :]
        f_up = pp_up[:, hs - 1:hs, :]
        b_lo = sb_lo[:, 0:1, :]
        b_up = sb_up[:, 0:1, :]
        pp = _merge(pp_lo, pp_up * f_lo)
        qq = _merge(qq_lo * f_up, qq_up)
        sb = _merge(sb_lo * b_up, sb_up)
        xb = _merge(xb_lo, xb_up * b_lo)
        hs *= 2
    return levels, pp, qq, sb, xb


def _hgrn_kernel(q_ref, v_ref, ff_ref, fb_ref, og_ref, lb_ref, nw_ref, o_ref,
                 oloc_ref, qt_ref, u_ref, d_ref, st_ref, *, chunk, layer):
    seq = q_ref.shape[2]
    n_chunks = seq // chunk
    t = chunk

    def lower_bound(logits):
        e = jnp.exp(logits - jnp.max(logits, axis=0, keepdims=True))
        return jnp.sum(e[0:layer + 1], axis=0, keepdims=True) / jnp.sum(e, axis=0, keepdims=True)

    lb_f = lower_bound(lb_ref[0])
    lb_b = lower_bound(lb_ref[1])

    rows_i = lax.broadcasted_iota(jnp.int32, (t, t), 0)
    cols_i = lax.broadcasted_iota(jnp.int32, (t, t), 1)
    pair_level = 31 - lax.clz(rows_i ^ cols_i)

    def chunk_body(c, carry):
        rows = pl.ds(pl.multiple_of(c * t, t), t)
        qr = q_ref[0, 0, rows, :].astype(F32)
        q = qr * _sigmoid(qr)
        v_bf = v_ref[0, 0, rows, :]
        v = v_bf.astype(F32)
        sf = _sigmoid(ff_ref[0, 0, rows, :].astype(F32))
        sbw = _sigmoid(fb_ref[0, 0, rows, :].astype(F32))
        ff = lb_f + (1.0 - lb_f) * sf
        fb = lb_b + (1.0 - lb_b) * sbw
        kf = (1.0 - lb_f) * (1.0 - sf)
        kb = (1.0 - lb_b) * (1.0 - sbw)

        levels, pp, qq, sb, xb = _hgrn_chunk(q, kf, ff, kb, fb)

        scores = jnp.zeros((t, t), F32)
        for lvl, (lhs, rhs) in enumerate(levels):
            scores = jnp.where(pair_level == lvl, _dot_nt(lhs, rhs), scores)
        diag = jnp.sum(q * (kf + kb), axis=1, keepdims=True)
        oloc_ref[rows, :] = _dot(scores.astype(BF16), v_bf) + diag * v

        qt_ref[rows, :] = jnp.concatenate([q * pp, q * sb], axis=1).astype(BF16)
        kt = jnp.concatenate([kf * qq, kb * xb], axis=1).astype(BF16)
        u_ref[c] = _dot_tn(v_bf, kt)
        decay = jnp.concatenate([pp[t - 1:t, :], sb[0:1, :]], axis=1)
        d_ref[c] = jnp.broadcast_to(decay, (SUBLANES, 2 * HEAD_DIM))
        return carry

    lax.fori_loop(0, n_chunks, chunk_body, 0)

    def fwd_body(c, s):
        st_ref[c, :, 0:HEAD_DIM] = s.astype(BF16)
        return s * d_ref[c, 0:1, 0:HEAD_DIM] + u_ref[c, :, 0:HEAD_DIM]

    def bwd_body(i, s):
        c = n_chunks - 1 - i
        st_ref[c, :, HEAD_DIM:2 * HEAD_DIM] = s.astype(BF16)
        return s * d_ref[c, 0:1, HEAD_DIM:2 * HEAD_DIM] + u_ref[c, :, HEAD_DIM:2 * HEAD_DIM]

    s0 = jnp.zeros((HEAD_DIM, HEAD_DIM), F32)
    lax.fori_loop(0, n_chunks, fwd_body, s0)
    lax.fori_loop(0, n_chunks, bwd_body, s0)

    def out_body(c, carry):
        rows = pl.ds(pl.multiple_of(c * t, t), t)
        o = oloc_ref[rows, :] + _dot_nt(qt_ref[rows, :], st_ref[c])
        o = o * lax.rsqrt(jnp.mean(o * o, axis=-1, keepdims=True) + LN_EPS) * nw_ref[...]
        og = og_ref[0, 0, rows, :].astype(F32)
        o_ref[0, rows, :] = (o * (og * _sigmoid(og))).astype(BF16)
        return carry

    lax.fori_loop(0, n_chunks, out_body, 0)


def _hgrn(p4d, lb_logits, norm_w, *, heads, layer):
    batch, _, seq, _ = p4d.shape
    chunk = min(256, seq)
    n_chunks = seq // chunk
    slots = lb_logits.shape[1]

    def col(group):
        return pl.BlockSpec((1, 1, seq, HEAD_DIM), lambda bi, h, g=group: (bi, g * heads + h, 0, 0))

    return pl.pallas_call(
        functools.partial(_hgrn_kernel, chunk=chunk, layer=layer),
        grid=(batch, heads),
        in_specs=[col(0), col(1), col(2), col(3), col(4),
                  pl.BlockSpec((2, slots, HEAD_DIM), lambda bi, h: (0, 0, h)),
                  pl.BlockSpec((1, HEAD_DIM), lambda bi, h: (0, 0))],
        out_specs=pl.BlockSpec((1, seq, HEAD_DIM), lambda bi, h: (bi, 0, h)),
        out_shape=jax.ShapeDtypeStruct((batch, seq, heads * HEAD_DIM), BF16),
        scratch_shapes=[
            pltpu.VMEM((seq, HEAD_DIM), F32),
            pltpu.VMEM((seq, 2 * HEAD_DIM), BF16),
            pltpu.VMEM((n_chunks, HEAD_DIM, 2 * HEAD_DIM), F32),
            pltpu.VMEM((n_chunks, SUBLANES, 2 * HEAD_DIM), F32),
            pltpu.VMEM((n_chunks, HEAD_DIM, 2 * HEAD_DIM), BF16),
        ],
        compiler_params=pltpu.CompilerParams(
            dimension_semantics=("arbitrary", "arbitrary"), vmem_limit_bytes=VMEM_LIMIT),
        name="hgrn",
    )(p4d, p4d, p4d, p4d, p4d, lb_logits, norm_w)


def _out_proj_kernel(x_ref, eg_ref, eb_ref, c_ref, r_ref, wc_ref, wr_ref, g_ref, b_ref, o_ref, *, alpha):
    h = _layer_norm(x_ref[...], eg_ref[...], eb_ref[...])
    mix = _dot(c_ref[...], wc_ref[...]) + _dot(r_ref[...], wr_ref[...])
    o_ref[...] = _layer_norm(alpha * h + mix, g_ref[...], b_ref[...])


def _out_proj(x2d, eg, eb, conv2d, rec2d, w_out, g, b, *, alpha):
    m, d = x2d.shape
    cw = conv2d.shape[1]
    rw = rec2d.shape[1]
    tm = min(512, m)
    vec = pl.BlockSpec((1, d), lambda i: (0, 0))
    return pl.pallas_call(
        functools.partial(_out_proj_kernel, alpha=alpha),
        grid=(m // tm,),
        in_specs=[
            pl.BlockSpec((tm, d), lambda i: (i, 0)), vec, vec,
            pl.BlockSpec((tm, cw), lambda i: (i, 0)),
            pl.BlockSpec((tm, rw), lambda i: (i, 0)),
            pl.BlockSpec((cw, d), lambda i: (0, 0)),
            pl.BlockSpec((rw, d), lambda i: (cw // rw, 0)),
            vec, vec,
        ],
        out_specs=pl.BlockSpec((tm, d), lambda i: (i, 0)),
        out_shape=jax.ShapeDtypeStruct((m, d), F32),
        compiler_params=pltpu.CompilerParams(
            dimension_semantics=("arbitrary",), vmem_limit_bytes=VMEM_LIMIT),
        name="out_proj",
    )(x2d, eg, eb, conv2d, rec2d, w_out, w_out, g, b)


def _ffn_kernel(h_ref, wu_ref, wd_ref, g_ref, b_ref, o_ref, hb_ref, acc_ref, *, alpha):
    f = pl.program_id(1)

    @pl.when(f == 0)
    def _():
        hb_ref[...] = h_ref[...].astype(BF16)
        acc_ref[...] = jnp.zeros_like(acc_ref)

    hid = jnp.maximum(_dot(hb_ref[...], wu_ref[...]), 0.0)
    acc_ref[...] += _dot((hid * hid).astype(BF16), wd_ref[...])

    @pl.when(f == pl.num_programs(1) - 1)
    def _():
        o_ref[...] = _layer_norm(alpha * h_ref[...] + acc_ref[...], g_ref[...], b_ref[...])


def _ffn(h2d, w_up, w_down, g, b, *, alpha):
    m, d = h2d.shape
    dff = w_up.shape[1]
    tm = min(512, m)
    tf = min(1024, dff)
    vec = pl.BlockSpec((1, d), lambda i, f: (0, 0))
    return pl.pallas_call(
        functools.partial(_ffn_kernel, alpha=alpha),
        grid=(m // tm, dff // tf),
        in_specs=[
            pl.BlockSpec((tm, d), lambda i, f: (i, 0)),
            pl.BlockSpec((d, tf), lambda i, f: (0, f)),
            pl.BlockSpec((tf, d), lambda i, f: (f, 0)),
            vec, vec,
        ],
        out_specs=pl.BlockSpec((tm, d), lambda i, f: (i, 0)),
        out_shape=jax.ShapeDtypeStruct((m, d), F32),
        scratch_shapes=[pltpu.VMEM((tm, d), BF16), pltpu.VMEM((tm, d), F32)],
        compiler_params=pltpu.CompilerParams(
            dimension_semantics=("arbitrary", "arbitrary"), vmem_limit_bytes=VMEM_LIMIT),
        name="ffn",
    )(h2d, w_up, w_down, g, b)


def kernel(x, emb_ln_g, emb_ln_b, lb_logits, w_in, conv_w, conv_b, conv_ln_g, conv_ln_b,
           hgrn_norm_w, w_out, ln1_g, ln1_b, w_up, w_down, ln2_g, ln2_b):
    batch, seq, d = x.shape
    depth = w_in.shape[0]
    assert depth == 1, "single-layer block only"
    conv_width = conv_w.shape[2]
    rec_width = d - conv_width
    heads = rec_width // HEAD_DIM
    assert lb_logits.shape[2] == heads * HEAD_DIM and hgrn_norm_w.shape[1] == HEAD_DIM
    assert w_in.shape[2] == 2 * conv_width + 5 * rec_width
    alpha = (2.0 * depth) ** 0.25
    row = lambda v: v.reshape(1, -1)

    x2d = x.reshape(batch * seq, d)
    u, proj = _ln_proj(x2d, row(emb_ln_g), row(emb_ln_b), w_in[0].astype(BF16),
                       batch=batch, seq=seq, conv_w=conv_width)
    conv_out = _conv(u.reshape(batch, seq, conv_width), conv_w[0], row(conv_b[0]),
                     row(conv_ln_g[0]), row(conv_ln_b[0]))
    rec_out = _hgrn(proj, lb_logits, row(hgrn_norm_w[0]), heads=heads, layer=0)
    h1 = _out_proj(x2d, row(emb_ln_g), row(emb_ln_b),
                   conv_out.reshape(batch * seq, conv_width), rec_out.reshape(batch * seq, rec_width),
                   w_out[0].astype(BF16), row(ln1_g[0]), row(ln1_b[0]), alpha=alpha)
    out = _ffn(h1, w_up[0].astype(BF16), w_down[0].astype(BF16), row(ln2_g[0]), row(ln2_b[0]), alpha=alpha)
    return out.reshape(batch, seq, d)
```
